```python
import math
import jax, jax.numpy as jnp
from jax import lax
import numpy as np

D_MODEL = 1024
BATCH = 8
SEQ = 4096
DEPTH = 1

CHUNK = 64
MIX_W = D_MODEL
FOX_W = MIX_W // 2
LRU_W = MIX_W - FOX_W
FOX_HEADS = 8
FOX_HD = FOX_W // FOX_HEADS
LRU_BLOCKS = 8
LRU_BW = LRU_W // LRU_BLOCKS
LRU_C = 8.0
CONV_K = 4
D_FF = 4 * D_MODEL
Q_BLOCK = 128
LN_EPS = 1e-5
DN_ALPHA = (2.0 * DEPTH) ** 0.25
DN_BETA = (8.0 * DEPTH) ** -0.25

Q_OFF = 0
K_OFF = Q_OFF + FOX_W
V_OFF = K_OFF + FOX_W
LX_OFF = V_OFF + FOX_W
LG_OFF = LX_OFF + LRU_W
FG_OFF = LG_OFF + LRU_W
IN_COLS = FG_OFF + FOX_HEADS

kernel_name = "fox_rglru_macaron_deepnorm_block"


def layer_norm(x, g, b):
    xf = x.astype(jnp.float32)
    mu = jnp.mean(xf, axis=-1, keepdims=True)
    var = jnp.mean(jnp.square(xf - mu), axis=-1, keepdims=True)
    y = (xf - mu) * lax.rsqrt(var + LN_EPS) * g.astype(jnp.float32) + b.astype(jnp.float32)
    return y.astype(x.dtype)


def swiglu(x, w_gate, w_up, w_down):
    return (jax.nn.silu(x @ w_gate) * (x @ w_up)) @ w_down


def forgetting_attention(q, k, v, fg_logit):
    seq = q.shape[1]
    scale = 1.0 / math.sqrt(FOX_HD)
    cum = jnp.cumsum(jax.nn.log_sigmoid(fg_logit.astype(jnp.float32)), axis=1)
    cum = cum.transpose(0, 2, 1)
    qh = q.transpose(0, 2, 1, 3)
    kh = k.transpose(0, 2, 1, 3)
    vh = v.transpose(0, 2, 1, 3)
    outs = []
    for i in range(seq // Q_BLOCK):
        q0, q1 = i * Q_BLOCK, (i + 1) * Q_BLOCK
        s = jnp.einsum('bhqd,bhkd->bhqk', qh[:, :, q0:q1], kh[:, :, :q1],
                       preferred_element_type=jnp.float32) * scale
        s = s + cum[:, :, q0:q1, None] - cum[:, :, None, :q1]
        mask = jnp.arange(q0, q1)[:, None] >= jnp.arange(q1)[None, :]
        s = jnp.where(mask, s, -1e30)
        p = jax.nn.softmax(s, axis=-1).astype(vh.dtype)
        outs.append(jnp.einsum('bhqk,bhkd->bqhd', p, vh[:, :, :q1]))
    return jnp.concatenate(outs, axis=1)


def causal_depthwise_conv(u, w, b):
    y = lax.conv_general_dilated(u, w[:, None, :], window_strides=(1,), padding=[(CONV_K - 1, 0)],
                                 dimension_numbers=('NWC', 'WIO', 'NWC'),
                                 feature_group_count=u.shape[-1])
    return y + b


def _lin_rec_combine(c1, c2):
    a1, b1 = c1
    a2, b2 = c2
    return a1 * a2, a2 * b1 + b2


def rg_lru(u, wa, ba, wx, bx, lam):
    bsz, seq, width = u.shape
    ub = u.reshape(bsz, seq, LRU_BLOCKS, LRU_BW)
    r = jax.nn.sigmoid(jnp.einsum('bshi,hij->bshj', ub, wa) + ba).reshape(bsz, seq, width)
    gi = jax.nn.sigmoid(jnp.einsum('bshi,hij->bshj', ub, wx) + bx).reshape(bsz, seq, width)
    log_a = -LRU_C * r.astype(jnp.float32) * jax.nn.softplus(-lam.astype(jnp.float32))
    a = jnp.exp(log_a)
    bterm = jnp.sqrt(-jnp.expm1(2.0 * log_a)) * (gi * u).astype(jnp.float32)
    _, h = lax.associative_scan(_lin_rec_combine, (a, bterm), axis=1)
    return h.astype(u.dtype)


def setup_inputs(seed: int = 0) -> dict:
    key = jax.random.key(seed)
    ks = iter(jax.random.split(key, 32))
    f32 = jnp.float32

    def nrm(shape, scale):
        return jax.random.normal(next(ks), shape, f32) * scale

    d_in, d_ff = D_MODEL ** -0.5, D_FF ** -0.5
    x = jax.random.normal(next(ks), (BATCH, SEQ, D_MODEL), f32)
    w_in = nrm((DEPTH, D_MODEL, IN_COLS), d_in)
    w_in = w_in.at[:, :, V_OFF:V_OFF + FOX_W].multiply(DN_BETA)
    a0 = jax.random.uniform(next(ks), (DEPTH, LRU_W), f32, 0.9, 0.999)
    p = a0 ** (1.0 / LRU_C)
    lru_lambda = jnp.log(p) - jnp.log1p(-p)
    return {
        "x": x,
        "ffn1_w_gate": nrm((DEPTH, D_MODEL, D_FF), d_in),
        "ffn1_w_up": nrm((DEPTH, D_MODEL, D_FF), d_in),
        "ffn1_w_down": nrm((DEPTH, D_FF, D_MODEL), d_ff * DN_BETA),
        "ln1_g": 1.0 + nrm((DEPTH, D_MODEL), 0.02),
        "ln1_b": nrm((DEPTH, D_MODEL), 0.02),
        "w_in": w_in,
        "b_forget": 3.0 + nrm((DEPTH, FOX_HEADS), 0.1),
        "conv_w": nrm((DEPTH, CONV_K, LRU_W), CONV_K ** -0.5),
        "conv_b": nrm((DEPTH, LRU_W), 0.02),
        "rg_wa": nrm((DEPTH, LRU_BLOCKS, LRU_BW, LRU_BW), LRU_BW ** -0.5),
        "rg_ba": nrm((DEPTH, LRU_BLOCKS, LRU_BW), 0.02),
        "rg_wx": nrm((DEPTH, LRU_BLOCKS, LRU_BW, LRU_BW), LRU_BW ** -0.5),
        "rg_bx": nrm((DEPTH, LRU_BLOCKS, LRU_BW), 0.02),
        "lru_lambda": lru_lambda,
        "w_out": nrm((DEPTH, MIX_W, D_MODEL), (MIX_W ** -0.5) * DN_BETA),
        "ln2_g": 1.0 + nrm((DEPTH, D_MODEL), 0.02),
        "ln2_b": nrm((DEPTH, D_MODEL), 0.02),
        "ffn2_w_gate": nrm((DEPTH, D_MODEL, D_FF), d_in),
        "ffn2_w_up": nrm((DEPTH, D_MODEL, D_FF), d_in),
        "ffn2_w_down": nrm((DEPTH, D_FF, D_MODEL), d_ff * DN_BETA),
        "ln3_g": 1.0 + nrm((DEPTH, D_MODEL), 0.02),
        "ln3_b": nrm((DEPTH, D_MODEL), 0.02),
    }


def reference(x, ffn1_w_gate, ffn1_w_up, ffn1_w_down, ln1_g, ln1_b, w_in, b_forget,
              conv_w, conv_b, rg_wa, rg_ba, rg_wx, rg_bx, lru_lambda, w_out,
              ln2_g, ln2_b, ffn2_w_gate, ffn2_w_up, ffn2_w_down, ln3_g, ln3_b):
    bsz, seq, _ = x.shape
    for l in range(DEPTH):
        x = layer_norm(DN_ALPHA * x + 0.5 * swiglu(x, ffn1_w_gate[l], ffn1_w_up[l], ffn1_w_down[l]),
                       ln1_g[l], ln1_b[l])
        z = x @ w_in[l]
        q = z[..., Q_OFF:Q_OFF + FOX_W].reshape(bsz, seq, FOX_HEADS, FOX_HD)
        k = z[..., K_OFF:K_OFF + FOX_W].reshape(bsz, seq, FOX_HEADS, FOX_HD)
        v = z[..., V_OFF:V_OFF + FOX_W].reshape(bsz, seq, FOX_HEADS, FOX_HD)
        fg = z[..., FG_OFF:FG_OFF + FOX_HEADS] + b_forget[l]
        fox = forgetting_attention(q, k, v, fg).reshape(bsz, seq, FOX_W)
        u = causal_depthwise_conv(z[..., LX_OFF:LX_OFF + LRU_W], conv_w[l], conv_b[l])
        rec = rg_lru(u, rg_wa[l], rg_ba[l], rg_wx[l], rg_bx[l], lru_lambda[l])
        lru = jax.nn.gelu(z[..., LG_OFF:LG_OFF + LRU_W]) * rec
        mix = jnp.concatenate([fox, lru], axis=-1) @ w_out[l]
        x = layer_norm(DN_ALPHA * x + mix, ln2_g[l], ln2_b[l])
        x = layer_norm(DN_ALPHA * x + 0.5 * swiglu(x, ffn2_w_gate[l], ffn2_w_up[l], ffn2_w_down[l]),
                       ln3_g[l], ln3_b[l])
    return x
```

```python
import functools
import math

import jax
import jax.numpy as jnp
from jax import lax
from jax.experimental import pallas as pl
from jax.experimental.pallas import tpu as pltpu

f32 = jnp.float32
bf16 = jnp.bfloat16

D_MODEL = 1024
DEPTH = 1
FOX_W = 512
LRU_W = 512
FOX_HEADS = 8
FOX_HD = 64
LRU_BLOCKS = 8
LRU_BW = 64
LRU_C = 8.0
CONV_K = 4
D_FF = 4 * D_MODEL
LN_EPS = 1e-5
DN_ALPHA = (2.0 * DEPTH) ** 0.25
IN_COLS = 3 * FOX_W + 2 * LRU_W + FOX_HEADS

LANES = 128
SUBLANES = 8
MXU_DIM = 256

LOG2E = 1.4426950408889634
NEG_BIG = -1e30

AUG_STRIDE = LANES // FOX_HEADS
IN_COLS_PAD = 3 * FOX_W + 2 * LRU_W + LANES

MIB = 1024 * 1024


def _layer_norm(y, g, b):
    mu = jnp.mean(y, axis=-1, keepdims=True)
    d = y - mu
    var = jnp.mean(d * d, axis=-1, keepdims=True)
    return d * lax.rsqrt(var + LN_EPS) * g + b


def _sigmoid(x):
    return 1.0 / (1.0 + jnp.exp(-x))


def _ffn_ln_kernel(x_ref, wg_ref, wu_ref, wd_ref, g_ref, b_ref, o_ref, xb_ref, acc_ref):
    j = pl.program_id(1)
    nj = pl.num_programs(1)

    @pl.when(j == 0)
    def _():
        xb_ref[...] = x_ref[...].astype(bf16)

    xb = xb_ref[...]
    gate = jnp.dot(xb, wg_ref[...], preferred_element_type=f32)
    up = jnp.dot(xb, wu_ref[...], preferred_element_type=f32)
    h = (gate * _sigmoid(gate) * up).astype(bf16)
    part = jnp.dot(h, wd_ref[...], preferred_element_type=f32)

    @pl.when(j == 0)
    def _():
        acc_ref[...] = part

    @pl.when(j > 0)
    def _():
        acc_ref[...] += part

    @pl.when(j == nj - 1)
    def _():
        y = DN_ALPHA * x_ref[...] + 0.5 * acc_ref[...]
        o_ref[...] = _layer_norm(y, g_ref[...], b_ref[...])


def _ffn_ln(x2d, wg, wu, wd, g, b, *, tm=1024, tf=1024):
    t, d = x2d.shape
    f = wg.shape[1]
    return pl.pallas_call(
        _ffn_ln_kernel,
        out_shape=jax.ShapeDtypeStruct((t, d), f32),
        grid=(t // tm, f // tf),
        in_specs=[
            pl.BlockSpec((tm, d), lambda i, j: (i, 0)),
            pl.BlockSpec((d, tf), lambda i, j: (0, j)),
            pl.BlockSpec((d, tf), lambda i, j: (0, j)),
            pl.BlockSpec((tf, d), lambda i, j: (j, 0)),
            pl.BlockSpec((1, d), lambda i, j: (0, 0)),
            pl.BlockSpec((1, d), lambda i, j: (0, 0)),
        ],
        out_specs=pl.BlockSpec((tm, d), lambda i, j: (i, 0)),
        scratch_shapes=[pltpu.VMEM((tm, d), bf16), pltpu.VMEM((tm, d), f32)],
        compiler_params=pltpu.CompilerParams(
            dimension_semantics=("parallel", "arbitrary"),
            vmem_limit_bytes=56 * MIB),
        name="ffn_ln",
    )(x2d, wg, wu, wd, g, b)


def _split3(x):
    hi = x.astype(bf16).astype(f32)
    r = x - hi
    mid = r.astype(bf16).astype(f32)
    lo = (r - mid).astype(bf16).astype(f32)
    return hi, mid, lo


def _pack3(x):
    hi, mid, lo = _split3(x)
    return (hi + pltpu.roll(mid, FOX_HEADS, axis=1) + pltpu.roll(lo, 2 * FOX_HEADS, axis=1)).astype(bf16)


def _in_proj_kernel(x_ref, w_ref, bf_ref, tri_ref, place_ref, ones_ref,
                    q_ref, k_ref, v_ref, lx_ref, lg_ref, aq_ref, ak_ref, carry_ref):
    s_idx = pl.program_id(1)

    @pl.when(s_idx == 0)
    def _():
        carry_ref[...] = jnp.zeros_like(carry_ref)

    xb = x_ref[0].astype(bf16)
    z = jnp.dot(xb, w_ref[...], preferred_element_type=f32)
    q_ref[0] = (z[:, 0:FOX_W] * (LOG2E / math.sqrt(FOX_HD))).astype(bf16)
    k_ref[0] = z[:, FOX_W:2 * FOX_W].astype(bf16)
    v_ref[0] = z[:, 2 * FOX_W:3 * FOX_W].astype(bf16)
    lx_ref[0] = z[:, 3 * FOX_W:3 * FOX_W + LRU_W]
    lg_ref[0] = z[:, 3 * FOX_W + LRU_W:3 * FOX_W + 2 * LRU_W]

    fg = z[:, 3 * FOX_W + 2 * LRU_W:] + bf_ref[...]
    lane = lax.broadcasted_iota(jnp.int32, fg.shape, 1)
    ls = (jnp.minimum(fg, 0.0) - jnp.log1p(jnp.exp(-jnp.abs(fg)))) * LOG2E
    ls = jnp.where(lane < FOX_HEADS, ls, 0.0)

    cum3 = jnp.dot(tri_ref[...], _pack3(ls), preferred_element_type=f32)
    cum = cum3 + pltpu.roll(cum3, LANES - FOX_HEADS, axis=1) + pltpu.roll(cum3, LANES - 2 * FOX_HEADS, axis=1)
    cum = jnp.where(lane < FOX_HEADS, cum, 0.0) + carry_ref[0:1, :]
    tm = cum.shape[0]
    carry_ref[...] = jnp.broadcast_to(cum[tm - 1:tm, :], carry_ref.shape)

    aug = jnp.dot(_pack3(cum), place_ref[...], preferred_element_type=f32)
    aq_ref[0] = (aug[:, :LANES] + ones_ref[0:1, :]).astype(bf16)
    ak_ref[0] = (aug[:, LANES:] + ones_ref[1:2, :]).astype(bf16)


def _in_proj(x1, w_in_p, bf_p, tri, place, ones, *, tm=512):
    b, s, d = x1.shape
    n = w_in_p.shape[1]
    const = lambda shape: pl.BlockSpec(shape, lambda bi, si: (0,) * len(shape))
    tok = lambda w: pl.BlockSpec((1, tm, w), lambda bi, si: (bi, si, 0))
    return pl.pallas_call(
        _in_proj_kernel,
        out_shape=(
            jax.ShapeDtypeStruct((b, s, FOX_W), bf16),
            jax.ShapeDtypeStruct((b, s, FOX_W), bf16),
            jax.ShapeDtypeStruct((b, s, FOX_W), bf16),
            jax.ShapeDtypeStruct((b, s, LRU_W), f32),
            jax.ShapeDtypeStruct((b, s, LRU_W), f32),
            jax.ShapeDtypeStruct((b, s, LANES), bf16),
            jax.ShapeDtypeStruct((b, s, LANES), bf16),
        ),
        grid=(b, s // tm),
        in_specs=[tok(d), const((d, n)), const((1, LANES)), const((tm, tm)),
                  const((LANES, 2 * LANES)), const((SUBLANES, LANES))],
        out_specs=(tok(FOX_W), tok(FOX_W), tok(FOX_W), tok(LRU_W), tok(LRU_W), tok(LANES), tok(LANES)),
        scratch_shapes=[pltpu.VMEM((SUBLANES, LANES), f32)],
        compiler_params=pltpu.CompilerParams(
            dimension_semantics=("arbitrary", "arbitrary"),
            vmem_limit_bytes=48 * MIB),
        name="in_proj",
    )(x1, w_in_p, bf_p, tri, place, ones)


def _neg_expm1(y):
    c = [1.0 / math.factorial(n) for n in range(1, 11)]
    poly = c[9]
    for n in range(8, -1, -1):
        poly = poly * y + c[n]
    return jnp.where(y > -0.25, -y * poly, 1.0 - jnp.exp(y))


def _gelu_tanh(x):
    return 0.5 * x * (1.0 + jnp.tanh(math.sqrt(2.0 / math.pi) * (x + 0.044715 * (x * x * x))))


def _rglru_kernel(lx_ref, lg_ref, cw_ref, cb_ref, wa_ref, ba_ref, wx_ref, bx_ref, lam_ref,
                  o_ref, xpad_ref, h_ref, a_ref, b_ref):
    t_idx = pl.program_id(1)
    tt = lx_ref.shape[1]
    pad = SUBLANES

    @pl.when(t_idx == 0)
    def _():
        xpad_ref[0:pad, :] = jnp.zeros((pad, LRU_W), f32)
        h_ref[...] = jnp.zeros_like(h_ref)

    xpad_ref[pad:pad + tt, :] = lx_ref[0]
    u = cb_ref[...] + cw_ref[CONV_K - 1:CONV_K, :] * xpad_ref[pad:pad + tt, :]
    for j in range(CONV_K - 1):
        off = pad - (CONV_K - 1) + j
        u = u + cw_ref[j:j + 1, :] * xpad_ref[off:off + tt, :]
    xpad_ref[0:pad, :] = xpad_ref[tt:tt + pad, :]

    ub = u.astype(bf16)
    half = LRU_W // 2

    def gate(w_ref, bias_ref):
        lo = jnp.dot(ub[:, :half], w_ref[0], preferred_element_type=f32)
        hi = jnp.dot(ub[:, half:], w_ref[1], preferred_element_type=f32)
        return _sigmoid(jnp.concatenate([lo, hi], axis=1) + bias_ref[...])

    r = gate(wa_ref, ba_ref)
    gi = gate(wx_ref, bx_ref)
    lam = lam_ref[...]
    softplus_neg_lam = jnp.maximum(-lam, 0.0) + jnp.log1p(jnp.exp(-jnp.abs(lam)))
    log_a = -LRU_C * r * softplus_neg_lam
    a = jnp.exp(log_a)
    bterm = jnp.sqrt(_neg_expm1(2.0 * log_a)) * (gi * u)

    a3 = a.reshape(tt // SUBLANES, SUBLANES, LRU_W)
    b3 = bterm.reshape(tt // SUBLANES, SUBLANES, LRU_W)
    row = lax.broadcasted_iota(jnp.int32, a3.shape, 1)
    for dist in (1, 2, 4):
        a_sh = pltpu.roll(a3, dist, axis=1)
        b_sh = pltpu.roll(b3, dist, axis=1)
        valid = row >= dist
        b3 = jnp.where(valid, a3 * b_sh + b3, b3)
        a3 = jnp.where(valid, a3 * a_sh, a3)
    a_ref[...] = a3.reshape(tt, LRU_W)
    b_ref[...] = b3.reshape(tt, LRU_W)

    def group(gi_, h):
        r0 = pl.multiple_of(gi_ * SUBLANES, SUBLANES)
        out = a_ref[pl.ds(r0, SUBLANES), :] * h + b_ref[pl.ds(r0, SUBLANES), :]
        b_ref[pl.ds(r0, SUBLANES), :] = out
        return jnp.broadcast_to(out[SUBLANES - 1:SUBLANES, :], (SUBLANES, LRU_W))

    h_ref[...] = lax.fori_loop(0, tt // SUBLANES, group, h_ref[...], unroll=8)
    o_ref[0] = (_gelu_tanh(lg_ref[0]) * b_ref[...]).astype(bf16)


def _rglru(lx, lg, cw, cb, wa_bd, ba, wx_bd, bx, lam, *, tt=512):
    b, s, w = lx.shape
    const = lambda shape: pl.BlockSpec(shape, lambda bi, ti: (0,) * len(shape))
    tok = pl.BlockSpec((1, tt, w), lambda bi, ti: (bi, ti, 0))
    return pl.pallas_call(
        _rglru_kernel,
        out_shape=jax.ShapeDtypeStruct((b, s, w), bf16),
        grid=(b, s // tt),
        in_specs=[tok, tok, const((CONV_K, w)), const((1, w)),
                  const((2, w // 2, w // 2)), const((1, w)),
                  const((2, w // 2, w // 2)), const((1, w)), const((1, w))],
        out_specs=tok,
        scratch_shapes=[pltpu.VMEM((tt + 2 * SUBLANES, w), f32), pltpu.VMEM((SUBLANES, w), f32),
                        pltpu.VMEM((tt, w), f32), pltpu.VMEM((tt, w), f32)],
        compiler_params=pltpu.CompilerParams(
            dimension_semantics=("arbitrary", "arbitrary"),
            vmem_limit_bytes=40 * MIB),
        name="rglru",
    )(lx, lg, cw, cb, wa_bd, ba, wx_bd, bx, lam)


def _fox_attn_kernel(q_ref, aq_ref, k_ref, ak_ref, v_ref, o_ref, m_ref, acc_ref):
    pair = pl.program_id(1)
    qi = pl.program_id(2)
    tq = q_ref.shape[1]
    tk = tq

    q2 = q_ref[0]
    aq = aq_ref[0]
    lane = lax.broadcasted_iota(jnp.int32, (tq, LANES), 1)
    lane_k = lax.broadcasted_iota(jnp.int32, (tk, LANES), 1)
    zero = jnp.zeros((), bf16)
    one = jnp.ones((), bf16)

    lhs = []
    for j in range(2):
        head = 2 * pair + j
        in_half = (lane >= j * FOX_HD) & (lane < (j + 1) * FOX_HD)
        in_head = (lane >= head * AUG_STRIDE) & (lane < (head + 1) * AUG_STRIDE)
        lhs.append(jnp.concatenate([jnp.where(in_half, q2, zero), jnp.where(in_head, aq, zero)], axis=1))
        m_ref[j] = jnp.full((tq, 1), NEG_BIG, f32)
        acc_ref[j] = jnp.zeros((tq, LANES), f32)

    def step(t, masked):
        k0 = pl.multiple_of(t * tk, tk)
        rhs = jnp.concatenate([k_ref[0, pl.ds(k0, tk), :], ak_ref[0, pl.ds(k0, tk), :]], axis=1)
        v2 = v_ref[0, pl.ds(k0, tk), :]
        for j in range(2):
            s = lax.dot_general(lhs[j], rhs, (((1,), (1,)), ((), ())), preferred_element_type=f32)
            if masked:
                row = lax.broadcasted_iota(jnp.int32, (tq, tk), 0)
                col = lax.broadcasted_iota(jnp.int32, (tq, tk), 1)
                s = jnp.where(row >= col, s, NEG_BIG)
            m_old = m_ref[j]
            m_new = jnp.maximum(m_old, jnp.max(s, axis=1, keepdims=True))
            p = jnp.exp2(s - m_new).astype(bf16)
            in_half_k = (lane_k >= j * FOX_HD) & (lane_k < (j + 1) * FOX_HD)
            vj = jnp.where(in_half_k, v2, one)
            acc_ref[j] = jnp.exp2(m_old - m_new) * acc_ref[j] + jnp.dot(p, vj, preferred_element_type=f32)
            m_ref[j] = m_new

    def full_step(t, c):
        step(t, False)
        return c

    lax.fori_loop(0, qi, full_step, 0)
    step(qi, True)

    outs = [acc_ref[j] / pltpu.roll(acc_ref[j], FOX_HD, axis=1) for j in range(2)]
    o_ref[0] = jnp.where(lane < FOX_HD, outs[0], outs[1]).astype(bf16)


def _fox_attn(q, aq, k, ak, v, *, tq=512):
    b, s, w = q.shape
    n_pairs = w // LANES
    qspec = pl.BlockSpec((1, tq, LANES), lambda bi, pi, qi: (bi, qi, pi))
    aqspec = pl.BlockSpec((1, tq, LANES), lambda bi, pi, qi: (bi, qi, 0))
    kvspec = pl.BlockSpec((1, s, LANES), lambda bi, pi, qi: (bi, 0, pi))
    akspec = pl.BlockSpec((1, s, LANES), lambda bi, pi, qi: (bi, 0, 0))
    return pl.pallas_call(
        _fox_attn_kernel,
        out_shape=jax.ShapeDtypeStruct((b, s, w), bf16),
        grid=(b, n_pairs, s // tq),
        in_specs=[qspec, aqspec, kvspec, akspec, kvspec],
        out_specs=qspec,
        scratch_shapes=[pltpu.VMEM((2, tq, 1), f32), pltpu.VMEM((2, tq, LANES), f32)],
        compiler_params=pltpu.CompilerParams(
            dimension_semantics=("parallel", "parallel", "arbitrary"),
            vmem_limit_bytes=40 * MIB),
        name="fox_attn",
    )(q, aq, k, ak, v)


def _out_ln_kernel(x_ref, fox_ref, lru_ref, wf_ref, wl_ref, g_ref, b_ref, o_ref):
    mix = jnp.dot(fox_ref[...], wf_ref[...], preferred_element_type=f32)
    mix = mix + jnp.dot(lru_ref[...], wl_ref[...], preferred_element_type=f32)
    o_ref[...] = _layer_norm(DN_ALPHA * x_ref[...] + mix, g_ref[...], b_ref[...])


def _out_ln(x2d, fox2d, lru2d, wf, wl, g, b, *, tm=1024):
    t, d = x2d.shape
    const = lambda shape: pl.BlockSpec(shape, lambda i: (0,) * len(shape))
    return pl.pallas_call(
        _out_ln_kernel,
        out_shape=jax.ShapeDtypeStruct((t, d), f32),
        grid=(t // tm,),
        in_specs=[pl.BlockSpec((tm, d), lambda i: (i, 0)),
                  pl.BlockSpec((tm, FOX_W), lambda i: (i, 0)),
                  pl.BlockSpec((tm, LRU_W), lambda i: (i, 0)),
                  const((FOX_W, d)), const((LRU_W, d)), const((1, d)), const((1, d))],
        out_specs=pl.BlockSpec((tm, d), lambda i: (i, 0)),
        compiler_params=pltpu.CompilerParams(
            dimension_semantics=("parallel",),
            vmem_limit_bytes=40 * MIB),
        name="out_ln",
    )(x2d, fox2d, lru2d, wf, wl, g, b)


def _block_diag_halves(w):
    nb = LRU_BLOCKS // 2
    halves = []
    for hidx in range(2):
        m = jnp.zeros((nb * LRU_BW, nb * LRU_BW), f32)
        for i in range(nb):
            m = lax.dynamic_update_slice(m, w[hidx * nb + i], (i * LRU_BW, i * LRU_BW))
        halves.append(m)
    return jnp.stack(halves).astype(bf16)


def _decay_constants(tm):
    tri = (jnp.arange(tm)[:, None] >= jnp.arange(tm)[None, :]).astype(bf16)
    src = jnp.arange(LANES)[:, None]
    col = jnp.arange(2 * LANES)[None, :]
    part, head = src // FOX_HEADS, src % FOX_HEADS
    q_hit = (src < 3 * FOX_HEADS) & (col == head * AUG_STRIDE + part)
    k_hit = (src < 3 * FOX_HEADS) & (col == LANES + head * AUG_STRIDE + 3 + part)
    place = (q_hit.astype(f32) - k_hit.astype(f32)).astype(bf16)
    slot = jnp.arange(LANES) % AUG_STRIDE
    ones = jnp.zeros((SUBLANES, LANES), f32)
    ones = ones.at[0].set(((slot >= 3) & (slot < 6)).astype(f32))
    ones = ones.at[1].set((slot < 3).astype(f32))
    return tri, place, ones


def kernel(x, ffn1_w_gate, ffn1_w_up, ffn1_w_down, ln1_g, ln1_b, w_in, b_forget, conv_w, conv_b,
           rg_wa, rg_ba, rg_wx, rg_bx, lru_lambda, w_out, ln2_g, ln2_b,
           ffn2_w_gate, ffn2_w_up, ffn2_w_down, ln3_g, ln3_b):
    bsz, seq, d = x.shape
    tm_proj = 512
    tri, place, ones = _decay_constants(tm_proj)
    h = x
    for l in range(DEPTH):
        row = lambda p: p[l].reshape(1, -1)
        x1 = _ffn_ln(h.reshape(bsz * seq, d), ffn1_w_gate[l].astype(bf16), ffn1_w_up[l].astype(bf16),
                     ffn1_w_down[l].astype(bf16), row(ln1_g), row(ln1_b))
        w_in_p = jnp.pad(w_in[l], ((0, 0), (0, IN_COLS_PAD - IN_COLS))).astype(bf16)
        bf_p = jnp.pad(b_forget[l], (0, LANES - FOX_HEADS)).reshape(1, LANES)
        q, k, v, lx, lg, aq, ak = _in_proj(x1.reshape(bsz, seq, d), w_in_p, bf_p, tri, place, ones,
                                           tm=tm_proj)
        lru = _rglru(lx, lg, conv_w[l], row(conv_b), _block_diag_halves(rg_wa[l]), row(rg_ba),
                     _block_diag_halves(rg_wx[l]), row(rg_bx), row(lru_lambda))
        fox = _fox_attn(q, aq, k, ak, v)
        w_out_b = w_out[l].astype(bf16)
        x2 = _out_ln(x1, fox.reshape(bsz * seq, FOX_W), lru.reshape(bsz * seq, LRU_W),
                     w_out_b[:FOX_W], w_out_b[FOX_W:], row(ln2_g), row(ln2_b))
        h = _ffn_ln(x2, ffn2_w_gate[l].astype(bf16), ffn2_w_up[l].astype(bf16),
                    ffn2_w_down[l].astype(bf16), row(ln3_g), row(ln3_b)).reshape(bsz, seq, d)
    return h
```

```python
import functools
import math

import jax
import jax.numpy as jnp
from jax import lax
from jax.experimental import pallas as pl
from jax.experimental.pallas import tpu as pltpu

f32 = jnp.float32
bf16 = jnp.bfloat16

D_MODEL = 1024
DEPTH = 1
FOX_W = 512
LRU_W = 512
FOX_HEADS = 8
FOX_HD = 64
LRU_BLOCKS = 8
LRU_BW = 64
LRU_C = 8.0
CONV_K = 4
D_FF = 4 * D_MODEL
LN_EPS = 1e-5
DN_ALPHA = (2.0 * DEPTH) ** 0.25
IN_COLS = 3 * FOX_W + 2 * LRU_W + FOX_HEADS

LANES = 128
SUBLANES = 8
MXU_DIM = 256

LOG2E = 1.4426950408889634
NEG_BIG = -1e30

AUG_STRIDE = LANES // FOX_HEADS
QUERY_BLOCK = MXU_DIM
IN_COLS_PAD = 3 * FOX_W + 2 * LRU_W + LANES

MIB = 1024 * 1024


def _layer_norm(y, g, b):
    mu = jnp.mean(y, axis=-1, keepdims=True)
    d = y - mu
    var = jnp.mean(d * d, axis=-1, keepdims=True)
    return d * lax.rsqrt(var + LN_EPS) * g + b


def _sigmoid(x):
    return 1.0 / (1.0 + jnp.exp(-x))


def _ffn_ln_kernel(x_ref, wg_ref, wu_ref, wd_ref, g_ref, b_ref, o_ref, yprev_ref, *, tc):
    i = pl.program_id(0)
    n = pl.num_programs(0)
    f = wg_ref.shape[1]

    @pl.when(i == 0)
    def _():
        yprev_ref[...] = jnp.zeros_like(yprev_ref)

    def finish_previous():
        o_ref[...] = _layer_norm(yprev_ref[...], g_ref[...], b_ref[...])

    @pl.when(i < n - 1)
    def _():
        finish_previous()
        x = x_ref[...]
        xb = x.astype(bf16)
        acc = None
        for c in range(f // tc):
            gate = jnp.dot(xb, wg_ref[:, c * tc:(c + 1) * tc], preferred_element_type=f32)
            up = jnp.dot(xb, wu_ref[:, c * tc:(c + 1) * tc], preferred_element_type=f32)
            h = (gate * _sigmoid(gate) * up).astype(bf16)
            part = jnp.dot(h, wd_ref[c * tc:(c + 1) * tc, :], preferred_element_type=f32)
            acc = part if acc is None else acc + part
        yprev_ref[...] = DN_ALPHA * x + 0.5 * acc

    @pl.when(i == n - 1)
    def _():
        finish_previous()


def _ffn_ln(x2d, wg, wu, wd, g, b, *, tm=1024, tc=512):
    t, d = x2d.shape
    f = wg.shape[1]
    n = t // tm
    resident = lambda shape: pl.BlockSpec(shape, lambda i: (0,) * len(shape), pipeline_mode=pl.Buffered(1))
    return pl.pallas_call(
        functools.partial(_ffn_ln_kernel, tc=tc),
        out_shape=jax.ShapeDtypeStruct((t, d), f32),
        grid=(n + 1,),
        in_specs=[
            pl.BlockSpec((tm, d), lambda i: (jnp.minimum(i, n - 1), 0)),
            resident((d, f)), resident((d, f)), resident((f, d)),
            resident((1, d)), resident((1, d)),
        ],
        out_specs=pl.BlockSpec((tm, d), lambda i: (jnp.maximum(i - 1, 0), 0)),
        scratch_shapes=[pltpu.VMEM((tm, d), f32)],
        compiler_params=pltpu.CompilerParams(
            dimension_semantics=("arbitrary",),
            vmem_limit_bytes=58 * MIB),
        name="ffn_ln",
    )(x2d, wg, wu, wd, g, b)


def _split3(x):
    hi = x.astype(bf16).astype(f32)
    r = x - hi
    mid = r.astype(bf16).astype(f32)
    lo = (r - mid).astype(bf16).astype(f32)
    return hi, mid, lo


def _pack3(x):
    hi, mid, lo = _split3(x)
    return (hi + pltpu.roll(mid, FOX_HEADS, axis=1) + pltpu.roll(lo, 2 * FOX_HEADS, axis=1)).astype(bf16)


Q_OFF, K_OFF, V_OFF = 0, FOX_W, 2 * FOX_W
LX_OFF, LG_OFF, FG_OFF = 3 * FOX_W, 3 * FOX_W + LRU_W, 3 * FOX_W + 2 * LRU_W


def _mixer_in_kernel(x_ref, w_ref, bf_ref, tri_ref, place_ref, ones_ref,
                     cw_ref, cb_ref, wa_ref, ba_ref, wx_ref, bx_ref, lam_ref,
                     q_ref, k_ref, vt_ref, aq_ref, ak_ref, lru_ref,
                     carry_ref, xpad_ref, h_ref, a_ref, b_ref, lg_ref):
    s_idx = pl.program_id(1)
    tm = x_ref.shape[1]
    pad = SUBLANES
    half = LRU_W // 2
    halves = [slice(hh * half, (hh + 1) * half) for hh in range(2)]

    @pl.when(s_idx == 0)
    def _():
        carry_ref[...] = jnp.zeros_like(carry_ref)
        xpad_ref[0:pad, :] = jnp.zeros((pad, LRU_W), f32)
        h_ref[...] = jnp.zeros_like(h_ref)

    xb = x_ref[0].astype(bf16)

    def project(col0, width):
        return jnp.dot(xb, w_ref[:, col0:col0 + width], preferred_element_type=f32)

    for hh in range(2):
        xpad_ref[pad:pad + tm, halves[hh]] = project(LX_OFF + hh * half, half)

    def conv(cols):
        u = cb_ref[:, cols] + cw_ref[CONV_K - 1:CONV_K, cols] * xpad_ref[pad:pad + tm, cols]
        for j in range(CONV_K - 1):
            off = pad - (CONV_K - 1) + j
            u = u + cw_ref[j:j + 1, cols] * xpad_ref[off:off + tm, cols]
        return u

    us = [conv(cols) for cols in halves]
    xpad_ref[0:pad, :] = xpad_ref[tm:tm + pad, :]

    ubs = [u.astype(bf16) for u in us]

    def gate(w_ref, bias_ref, hh):
        cols = halves[hh]
        return jnp.tanh(jnp.dot(ubs[hh], w_ref[cols, cols], preferred_element_type=f32)
                        + bias_ref[:, cols])

    def scan_groups(hh, t_r, t_gi):
        cols = halves[hh]
        lam = lam_ref[:, cols]
        softplus_neg_lam = jnp.maximum(-lam, 0.0) + jnp.log1p(jnp.exp(-jnp.abs(lam)))
        half_c = (-0.5 * LRU_C) * softplus_neg_lam
        log_a = half_c * t_r + half_c
        gi = 0.5 * t_gi + 0.5
        a = jnp.exp(log_a)
        one_minus_a2 = _neg_expm1(2.0 * log_a, a)
        root = jnp.where(one_minus_a2 > 0.0, one_minus_a2 * lax.rsqrt(one_minus_a2), 0.0)
        bterm = root * (gi * us[hh])
        a3 = a.reshape(tm // SUBLANES, SUBLANES, half)
        b3 = bterm.reshape(tm // SUBLANES, SUBLANES, half)
        row = lax.broadcasted_iota(jnp.int32, a3.shape, 1)
        for dist in (1, 2, 4):
            a_sh = pltpu.roll(a3, dist, axis=1)
            b_sh = pltpu.roll(b3, dist, axis=1)
            valid = row >= dist
            b3 = jnp.where(valid, a3 * b_sh + b3, b3)
            a3 = jnp.where(valid, a3 * a_sh, a3)
        a_ref[:, cols] = a3.reshape(tm, half)
        b_ref[:, cols] = b3.reshape(tm, half)

    r0 = gate(wa_ref, ba_ref, 0)
    q_ref[0] = (project(Q_OFF, FOX_W) * (LOG2E / math.sqrt(FOX_HD))).astype(bf16)
    gi0 = gate(wx_ref, bx_ref, 0)
    lg_ref[...] = project(LG_OFF, LRU_W)
    r1 = gate(wa_ref, ba_ref, 1)

    fg = project(FG_OFF, LANES) + bf_ref[...]
    lane = lax.broadcasted_iota(jnp.int32, fg.shape, 1)
    ls = (jnp.minimum(fg, 0.0) - jnp.log1p(jnp.exp(-jnp.abs(fg)))) * LOG2E
    ls = jnp.where(lane < FOX_HEADS, ls, 0.0)

    k_ref[0] = project(K_OFF, FOX_W).astype(bf16)
    gi1 = gate(wx_ref, bx_ref, 1)

    cum3 = jnp.dot(tri_ref[...], _pack3(ls), preferred_element_type=f32)
    cum = cum3 + pltpu.roll(cum3, LANES - FOX_HEADS, axis=1) + pltpu.roll(cum3, LANES - 2 * FOX_HEADS, axis=1)
    cum = jnp.where(lane < FOX_HEADS, cum, 0.0) + carry_ref[0:1, :]
    carry_ref[...] = jnp.broadcast_to(cum[tm - 1:tm, :], carry_ref.shape)

    vt_ref[0] = project(V_OFF, FOX_W).T.astype(bf16)

    aug = jnp.dot(_pack3(cum), place_ref[...], preferred_element_type=f32)
    aq_ref[0] = (aug[:, :LANES] + ones_ref[0:1, :]).astype(bf16)
    ak_ref[0] = (aug[:, LANES:] + ones_ref[1:2, :]).astype(bf16)

    scan_groups(0, r0, gi0)
    scan_groups(1, r1, gi1)

    def group(gi_, h):
        r0_ = pl.multiple_of(gi_ * SUBLANES, SUBLANES)
        out = a_ref[pl.ds(r0_, SUBLANES), :] * h + b_ref[pl.ds(r0_, SUBLANES), :]
        b_ref[pl.ds(r0_, SUBLANES), :] = out
        return jnp.broadcast_to(out[SUBLANES - 1:SUBLANES, :], (SUBLANES, LRU_W))

    h_ref[...] = lax.fori_loop(0, tm // SUBLANES, group, h_ref[...], unroll=8)
    lru_ref[0] = (_gelu_tanh(lg_ref[...]) * b_ref[...]).astype(bf16)


def _mixer_in(x1, w_in_p, bf_p, tri, place, ones, cw, cb, wa_bd, ba, wx_bd, bx, lam, *, tm=512):
    b, s, d = x1.shape
    n = w_in_p.shape[1]
    w = LRU_W
    const = lambda shape: pl.BlockSpec(shape, lambda bi, si: (0,) * len(shape))
    tok = lambda width: pl.BlockSpec((1, tm, width), lambda bi, si: (bi, si, 0))
    return pl.pallas_call(
        _mixer_in_kernel,
        out_shape=(
            jax.ShapeDtypeStruct((b, s, FOX_W), bf16),
            jax.ShapeDtypeStruct((b, s, FOX_W), bf16),
            jax.ShapeDtypeStruct((b, FOX_W, s), bf16),
            jax.ShapeDtypeStruct((b, s, LANES), bf16),
            jax.ShapeDtypeStruct((b, s, LANES), bf16),
            jax.ShapeDtypeStruct((b, s, w), bf16),
        ),
        grid=(b, s // tm),
        in_specs=[tok(d), const((d, n)), const((1, LANES)), const((tm, tm)),
                  const((LANES, 2 * LANES)), const((SUBLANES, LANES)),
                  const((CONV_K, w)), const((1, w)), const((w, w)), const((1, w)),
                  const((w, w)), const((1, w)), const((1, w))],
        out_specs=(tok(FOX_W), tok(FOX_W), pl.BlockSpec((1, FOX_W, tm), lambda bi, si: (bi, 0, si)),
                   tok(LANES), tok(LANES), tok(w)),
        scratch_shapes=[pltpu.VMEM((SUBLANES, LANES), f32),
                        pltpu.VMEM((tm + 2 * SUBLANES, w), f32), pltpu.VMEM((SUBLANES, w), f32),
                        pltpu.VMEM((tm, w), f32), pltpu.VMEM((tm, w), f32), pltpu.VMEM((tm, w), f32)],
        compiler_params=pltpu.CompilerParams(
            dimension_semantics=("arbitrary", "arbitrary"),
            vmem_limit_bytes=48 * MIB),
        name="mixer_in",
    )(x1, w_in_p, bf_p, tri, place, ones, cw, cb, wa_bd, ba, wx_bd, bx, lam)


EXPM1_SERIES_BOUND = 0.0625


def _neg_expm1(y, exp_half_y):
    c = [-1.0 / math.factorial(n) for n in range(1, 6)]
    poly = c[4]
    for n in range(3, -1, -1):
        poly = poly * y + c[n]
    return jnp.where(y > -EXPM1_SERIES_BOUND, y * poly, 1.0 - exp_half_y * exp_half_y)


def _gelu_tanh(x):
    return 0.5 * x * (1.0 + jnp.tanh(math.sqrt(2.0 / math.pi) * (x + 0.044715 * (x * x * x))))


def _fox_attn_kernel(q_ref, aq_ref, k_ref, ak_ref, vt_ref, o_ref, m_ref, acc_ref):
    pair = pl.program_id(1)
    qi = pl.program_id(2)
    tq = q_ref.shape[1]
    tk = tq

    q2 = q_ref[0]
    aq = aq_ref[0]
    lane = lax.broadcasted_iota(jnp.int32, (tq, LANES), 1)
    zero = jnp.zeros((), bf16)
    one = jnp.ones((), bf16)

    qa = []
    for j in range(2):
        head = 2 * pair + j
        in_half = (lane >= j * FOX_HD) & (lane < (j + 1) * FOX_HD)
        in_head = (lane >= head * AUG_STRIDE) & (lane < (head + 1) * AUG_STRIDE)
        qa.append(jnp.concatenate([jnp.where(in_half, q2, zero), jnp.where(in_head, aq, zero)], axis=1).T)
        m_ref[j] = jnp.full((1, tq), NEG_BIG, f32)
        acc_ref[j] = jnp.zeros((LANES, tq), f32)

    chains = [(j, c) for j in range(2) for c in range(tq // QUERY_BLOCK)]

    def steps(tiles):
        todo = [(ti, j, c) for ti in range(len(tiles)) for j, c in chains]
        loaded = {}

        def scores(item):
            ti, j, c = item
            t, masked = tiles[ti]
            nk = min(tk, (c + 1) * QUERY_BLOCK) if masked else tk
            if (ti, nk) not in loaded:
                k0 = pl.multiple_of(t * tk, tk)
                ka = jnp.concatenate([k_ref[0, pl.ds(k0, nk), :], ak_ref[0, pl.ds(k0, nk), :]], axis=1)
                loaded[(ti, nk)] = (ka, vt_ref[0, :, pl.ds(k0, nk)])
            ka, vt2 = loaded[(ti, nk)]
            st = jnp.dot(ka, qa[j][:, c * QUERY_BLOCK:(c + 1) * QUERY_BLOCK], preferred_element_type=f32)
            return j, c, masked, vt2, st

        for j, c, masked, vt2, st in [scores(item) for item in todo]:
            cols = slice(c * QUERY_BLOCK, (c + 1) * QUERY_BLOCK)
            if masked:
                key = lax.broadcasted_iota(jnp.int32, st.shape, 0)
                qry = lax.broadcasted_iota(jnp.int32, st.shape, 1) + c * QUERY_BLOCK
                st = jnp.where(qry >= key, st, NEG_BIG)
            m_old = m_ref[j, :, cols]
            m_new = jnp.maximum(m_old, jnp.max(st, axis=0, keepdims=True))
            pt = jnp.exp2(st - m_new).astype(bf16)
            vrow = lax.broadcasted_iota(jnp.int32, vt2.shape, 0)
            own = (vrow >= j * FOX_HD) & (vrow < (j + 1) * FOX_HD)
            vtj = jnp.where(own, vt2, one)
            acc_ref[j, :, cols] = (jnp.exp2(m_old - m_new) * acc_ref[j, :, cols]
                                   + jnp.dot(vtj, pt, preferred_element_type=f32))
            m_ref[j, :, cols] = m_new

    def two_full_steps(u, carry):
        steps([(2 * u, False), (2 * u + 1, False)])
        return carry

    lax.fori_loop(0, qi // 2, two_full_steps, 0)

    @pl.when(qi % 2 == 1)
    def _():
        steps([(qi - 1, False), (qi, True)])

    @pl.when(qi % 2 == 0)
    def _():
        steps([(qi, True)])

    orow = lax.broadcasted_iota(jnp.int32, (LANES, tq), 0)
    outs = [acc_ref[j] / pltpu.roll(acc_ref[j], FOX_HD, axis=0) for j in range(2)]
    o_ref[0] = jnp.where(orow < FOX_HD, outs[0], outs[1]).T.astype(bf16)


def _fox_attn(q, aq, k, ak, vt, *, tq=512):
    b, s, w = q.shape
    n_pairs = w // LANES
    qspec = pl.BlockSpec((1, tq, LANES), lambda bi, pi, qi: (bi, qi, pi))
    aqspec = pl.BlockSpec((1, tq, LANES), lambda bi, pi, qi: (bi, qi, 0))
    kspec = pl.BlockSpec((1, s, LANES), lambda bi, pi, qi: (bi, 0, pi))
    akspec = pl.BlockSpec((1, s, LANES), lambda bi, pi, qi: (bi, 0, 0))
    vtspec = pl.BlockSpec((1, LANES, s), lambda bi, pi, qi: (bi, pi, 0))
    return pl.pallas_call(
        _fox_attn_kernel,
        out_shape=jax.ShapeDtypeStruct((b, s, w), bf16),
        grid=(b, n_pairs, s // tq),
        in_specs=[qspec, aqspec, kspec, akspec, vtspec],
        out_specs=qspec,
        scratch_shapes=[pltpu.VMEM((2, 1, tq), f32), pltpu.VMEM((2, LANES, tq), f32)],
        compiler_params=pltpu.CompilerParams(
            dimension_semantics=("parallel", "parallel", "arbitrary"),
            vmem_limit_bytes=40 * MIB),
        name="fox_attn",
    )(q, aq, k, ak, vt)


def _out_ln_kernel(x_ref, fox_ref, lru_ref, wf_ref, wl_ref, g_ref, b_ref, o_ref):
    mix = jnp.dot(fox_ref[...], wf_ref[...], preferred_element_type=f32)
    mix = mix + jnp.dot(lru_ref[...], wl_ref[...], preferred_element_type=f32)
    o_ref[...] = _layer_norm(DN_ALPHA * x_ref[...] + mix, g_ref[...], b_ref[...])


def _out_ln(x2d, fox2d, lru2d, wf, wl, g, b, *, tm=2048):
    t, d = x2d.shape
    const = lambda shape: pl.BlockSpec(shape, lambda i: (0,) * len(shape))
    return pl.pallas_call(
        _out_ln_kernel,
        out_shape=jax.ShapeDtypeStruct((t, d), f32),
        grid=(t // tm,),
        in_specs=[pl.BlockSpec((tm, d), lambda i: (i, 0)),
                  pl.BlockSpec((tm, FOX_W), lambda i: (i, 0)),
                  pl.BlockSpec((tm, LRU_W), lambda i: (i, 0)),
                  const((FOX_W, d)), const((LRU_W, d)), const((1, d)), const((1, d))],
        out_specs=pl.BlockSpec((tm, d), lambda i: (i, 0)),
        compiler_params=pltpu.CompilerParams(
            dimension_semantics=("parallel",),
            vmem_limit_bytes=56 * MIB),
        name="out_ln",
    )(x2d, fox2d, lru2d, wf, wl, g, b)


def _block_diag(w):
    m = jnp.zeros((LRU_W, LRU_W), f32)
    for i in range(LRU_BLOCKS):
        m = lax.dynamic_update_slice(m, w[i], (i * LRU_BW, i * LRU_BW))
    return m.astype(bf16)


def _decay_constants(tm):
    tri = (jnp.arange(tm)[:, None] >= jnp.arange(tm)[None, :]).astype(bf16)
    src = jnp.arange(LANES)[:, None]
    col = jnp.arange(2 * LANES)[None, :]
    part, head = src // FOX_HEADS, src % FOX_HEADS
    q_hit = (src < 3 * FOX_HEADS) & (col == head * AUG_STRIDE + part)
    k_hit = (src < 3 * FOX_HEADS) & (col == LANES + head * AUG_STRIDE + 3 + part)
    place = (q_hit.astype(f32) - k_hit.astype(f32)).astype(bf16)
    slot = jnp.arange(LANES) % AUG_STRIDE
    ones = jnp.zeros((SUBLANES, LANES), f32)
    ones = ones.at[0].set(((slot >= 3) & (slot < 6)).astype(f32))
    ones = ones.at[1].set((slot < 3).astype(f32))
    return tri, place, ones


def kernel(x, ffn1_w_gate, ffn1_w_up, ffn1_w_down, ln1_g, ln1_b, w_in, b_forget, conv_w, conv_b,
           rg_wa, rg_ba, rg_wx, rg_bx, lru_lambda, w_out, ln2_g, ln2_b,
           ffn2_w_gate, ffn2_w_up, ffn2_w_down, ln3_g, ln3_b):
    bsz, seq, d = x.shape
    tm_proj = 512
    tri, place, ones = _decay_constants(tm_proj)
    h = x
    for l in range(DEPTH):
        row = lambda p: p[l].reshape(1, -1)
        x1 = _ffn_ln(h.reshape(bsz * seq, d), ffn1_w_gate[l].astype(bf16), ffn1_w_up[l].astype(bf16),
                     ffn1_w_down[l].astype(bf16), row(ln1_g), row(ln1_b))
        w_in_p = jnp.pad(w_in[l], ((0, 0), (0, IN_COLS_PAD - IN_COLS))).astype(bf16)
        bf_p = jnp.pad(b_forget[l], (0, LANES - FOX_HEADS)).reshape(1, LANES)
        q, k, vt, aq, ak, lru = _mixer_in(
            x1.reshape(bsz, seq, d), w_in_p, bf_p, tri, place, ones, conv_w[l], row(conv_b),
            _block_diag(0.5 * rg_wa[l]), 0.5 * row(rg_ba), _block_diag(0.5 * rg_wx[l]), 0.5 * row(rg_bx),
            row(lru_lambda), tm=tm_proj)
        fox = _fox_attn(q, aq, k, ak, vt)
        w_out_b = w_out[l].astype(bf16)
        x2 = _out_ln(x1, fox.reshape(bsz * seq, FOX_W), lru.reshape(bsz * seq, LRU_W),
                     w_out_b[:FOX_W], w_out_b[FOX_W:], row(ln2_g), row(ln2_b))
        h = _ffn_ln(x2, ffn2_w_gate[l].astype(bf16), ffn2_w_up[l].astype(bf16),
                    ffn2_w_down[l].astype(bf16), row(ln3_g), row(ln3_b)).reshape(bsz, seq, d)
    return h
```

```python
import functools
import math

import jax
import jax.numpy as jnp
from jax import lax
from jax.experimental import pallas as pl
from jax.experimental.pallas import tpu as pltpu

f32 = jnp.float32
bf16 = jnp.bfloat16

D_MODEL = 1024
DEPTH = 1
FOX_W = 512
LRU_W = 512
FOX_HEADS = 8
FOX_HD = 64
LRU_BLOCKS = 8
LRU_BW = 64
LRU_C = 8.0
CONV_K = 4
D_FF = 4 * D_MODEL
LN_EPS = 1e-5
DN_ALPHA = (2.0 * DEPTH) ** 0.25
IN_COLS = 3 * FOX_W + 2 * LRU_W + FOX_HEADS

LANES = 128
SUBLANES = 8
MXU_DIM = 256

LOG2E = 1.4426950408889634
NEG_BIG = -1e30

AUG_STRIDE = LANES // FOX_HEADS
QUERY_BLOCK = MXU_DIM
IN_COLS_PAD = 3 * FOX_W + 2 * LRU_W + LANES

MIB = 1024 * 1024


def _layer_norm(y, g, b):
    mu = jnp.mean(y, axis=-1, keepdims=True)
    d = y - mu
    var = jnp.mean(d * d, axis=-1, keepdims=True)
    return d * lax.rsqrt(var + LN_EPS) * g + b


def _sigmoid(x):
    return 1.0 / (1.0 + jnp.exp(-x))


def _ffn_ln_kernel(x_ref, wg_ref, wu_ref, wd_ref, g_ref, b_ref, o_ref, yprev_ref, *, tc):
    i = pl.program_id(0)
    n = pl.num_programs(0)
    f = wg_ref.shape[1]

    @pl.when(i == 0)
    def _():
        yprev_ref[...] = jnp.zeros_like(yprev_ref)

    def finish_previous():
        o_ref[...] = _layer_norm(yprev_ref[...], g_ref[...], b_ref[...])

    @pl.when(i < n - 1)
    def _():
        finish_previous()
        x = x_ref[...]
        xb = x.astype(bf16)
        acc = None
        for c in range(f // tc):
            gate = jnp.dot(xb, wg_ref[:, c * tc:(c + 1) * tc], preferred_element_type=f32)
            up = jnp.dot(xb, wu_ref[:, c * tc:(c + 1) * tc], preferred_element_type=f32)
            h = (gate * _sigmoid(gate) * up).astype(bf16)
            part = jnp.dot(h, wd_ref[c * tc:(c + 1) * tc, :], preferred_element_type=f32)
            acc = part if acc is None else acc + part
        yprev_ref[...] = DN_ALPHA * x + 0.5 * acc

    @pl.when(i == n - 1)
    def _():
        finish_previous()


def _ffn_ln(x2d, wg, wu, wd, g, b, *, tm=1024, tc=512):
    t, d = x2d.shape
    f = wg.shape[1]
    n = t // tm
    resident = lambda shape: pl.BlockSpec(shape, lambda i: (0,) * len(shape), pipeline_mode=pl.Buffered(1))
    return pl.pallas_call(
        functools.partial(_ffn_ln_kernel, tc=tc),
        out_shape=jax.ShapeDtypeStruct((t, d), f32),
        grid=(n + 1,),
        in_specs=[
            pl.BlockSpec((tm, d), lambda i: (jnp.minimum(i, n - 1), 0)),
            resident((d, f)), resident((d, f)), resident((f, d)),
            resident((1, d)), resident((1, d)),
        ],
        out_specs=pl.BlockSpec((tm, d), lambda i: (jnp.maximum(i - 1, 0), 0)),
        scratch_shapes=[pltpu.VMEM((tm, d), f32)],
        compiler_params=pltpu.CompilerParams(
            dimension_semantics=("arbitrary",),
            vmem_limit_bytes=58 * MIB),
        name="ffn_ln",
    )(x2d, wg, wu, wd, g, b)


def _split3(x):
    hi = x.astype(bf16).astype(f32)
    r = x - hi
    mid = r.astype(bf16).astype(f32)
    lo = (r - mid).astype(bf16).astype(f32)
    return hi, mid, lo


def _pack3(x):
    hi, mid, lo = _split3(x)
    return (hi + pltpu.roll(mid, FOX_HEADS, axis=1) + pltpu.roll(lo, 2 * FOX_HEADS, axis=1)).astype(bf16)


Q_OFF, K_OFF, V_OFF = 0, FOX_W, 2 * FOX_W
LX_OFF, LG_OFF, FG_OFF = 3 * FOX_W, 3 * FOX_W + LRU_W, 3 * FOX_W + 2 * LRU_W


def _mixer_in_kernel(x_ref, w_ref, bf_ref, tri_ref, place_ref, ones_ref,
                     cw_ref, cb_ref, wa_ref, ba_ref, wx_ref, bx_ref, lam_ref,
                     q_ref, k_ref, vt_ref, aq_ref, ak_ref, lru_ref,
                     carry_ref, xpad_ref, h_ref, a_ref, b_ref, lg_ref):
    s_idx = pl.program_id(1)
    tm = x_ref.shape[1]
    pad = SUBLANES
    half = LRU_W // 2
    halves = [slice(hh * half, (hh + 1) * half) for hh in range(2)]

    @pl.when(s_idx == 0)
    def _():
        carry_ref[...] = jnp.zeros_like(carry_ref)
        xpad_ref[0:pad, :] = jnp.zeros((pad, LRU_W), f32)
        h_ref[...] = jnp.zeros_like(h_ref)

    xb = x_ref[0].astype(bf16)

    def project(col0, width):
        return jnp.dot(xb, w_ref[:, col0:col0 + width], preferred_element_type=f32)

    for hh in range(2):
        xpad_ref[pad:pad + tm, halves[hh]] = project(LX_OFF + hh * half, half)

    def conv(cols):
        u = cb_ref[:, cols] + cw_ref[CONV_K - 1:CONV_K, cols] * xpad_ref[pad:pad + tm, cols]
        for j in range(CONV_K - 1):
            off = pad - (CONV_K - 1) + j
            u = u + cw_ref[j:j + 1, cols] * xpad_ref[off:off + tm, cols]
        return u

    us = [conv(cols) for cols in halves]
    xpad_ref[0:pad, :] = xpad_ref[tm:tm + pad, :]

    ubs = [u.astype(bf16) for u in us]

    def gate(w_ref, bias_ref, hh):
        cols = halves[hh]
        return jnp.tanh(jnp.dot(ubs[hh], w_ref[cols, cols], preferred_element_type=f32)
                        + bias_ref[:, cols])

    def scan_groups(hh, t_r, t_gi):
        cols = halves[hh]
        lam = lam_ref[:, cols]
        softplus_neg_lam = jnp.maximum(-lam, 0.0) + jnp.log1p(jnp.exp(-jnp.abs(lam)))
        half_c = (-0.5 * LRU_C) * softplus_neg_lam
        log_a = half_c * t_r + half_c
        gi = 0.5 * t_gi + 0.5
        a = jnp.exp(log_a)
        one_minus_a2 = _neg_expm1(2.0 * log_a, a)
        root = jnp.where(one_minus_a2 > 0.0, one_minus_a2 * lax.rsqrt(one_minus_a2), 0.0)
        bterm = root * (gi * us[hh])
        a3 = a.reshape(tm // SUBLANES, SUBLANES, half)
        b3 = bterm.reshape(tm // SUBLANES, SUBLANES, half)
        row = lax.broadcasted_iota(jnp.int32, a3.shape, 1)
        for dist in (1, 2, 4):
            a_sh = pltpu.roll(a3, dist, axis=1)
            b_sh = pltpu.roll(b3, dist, axis=1)
            valid = row >= dist
            b3 = jnp.where(valid, a3 * b_sh + b3, b3)
            a3 = jnp.where(valid, a3 * a_sh, a3)
        a_ref[:, cols] = a3.reshape(tm, half)
        b_ref[:, cols] = b3.reshape(tm, half)

    r0 = gate(wa_ref, ba_ref, 0)
    q_ref[0] = (project(Q_OFF, FOX_W) * (LOG2E / math.sqrt(FOX_HD))).astype(bf16)
    gi0 = gate(wx_ref, bx_ref, 0)
    lg_ref[...] = project(LG_OFF, LRU_W)
    r1 = gate(wa_ref, ba_ref, 1)

    fg = project(FG_OFF, LANES) + bf_ref[...]
    lane = lax.broadcasted_iota(jnp.int32, fg.shape, 1)
    ls = (jnp.minimum(fg, 0.0) - jnp.log1p(jnp.exp(-jnp.abs(fg)))) * LOG2E
    ls = jnp.where(lane < FOX_HEADS, ls, 0.0)

    k_ref[0] = project(K_OFF, FOX_W).astype(bf16)
    gi1 = gate(wx_ref, bx_ref, 1)

    cum3 = jnp.dot(tri_ref[...], _pack3(ls), preferred_element_type=f32)
    cum = cum3 + pltpu.roll(cum3, LANES - FOX_HEADS, axis=1) + pltpu.roll(cum3, LANES - 2 * FOX_HEADS, axis=1)
    cum = jnp.where(lane < FOX_HEADS, cum, 0.0) + carry_ref[0:1, :]
    carry_ref[...] = jnp.broadcast_to(cum[tm - 1:tm, :], carry_ref.shape)

    vt_ref[0] = project(V_OFF, FOX_W).T.astype(bf16)

    aug = jnp.dot(_pack3(cum), place_ref[...], preferred_element_type=f32)
    aq_ref[0] = (aug[:, :LANES] + ones_ref[0:1, :]).astype(bf16)
    ak_ref[0] = (aug[:, LANES:] + ones_ref[1:2, :]).astype(bf16)

    scan_groups(0, r0, gi0)
    scan_groups(1, r1, gi1)

    def group(gi_, h):
        r0_ = pl.multiple_of(gi_ * SUBLANES, SUBLANES)
        out = a_ref[pl.ds(r0_, SUBLANES), :] * h + b_ref[pl.ds(r0_, SUBLANES), :]
        b_ref[pl.ds(r0_, SUBLANES), :] = out
        return jnp.broadcast_to(out[SUBLANES - 1:SUBLANES, :], (SUBLANES, LRU_W))

    h_ref[...] = lax.fori_loop(0, tm // SUBLANES, group, h_ref[...], unroll=8)
    lru_ref[0] = (_gelu_tanh(lg_ref[...]) * b_ref[...]).astype(bf16)


def _mixer_in(x1, w_in_p, bf_p, tri, place, ones, cw, cb, wa_bd, ba, wx_bd, bx, lam, *, tm=512):
    b, s, d = x1.shape
    n = w_in_p.shape[1]
    w = LRU_W
    const = lambda shape: pl.BlockSpec(shape, lambda bi, si: (0,) * len(shape))
    tok = lambda width: pl.BlockSpec((1, tm, width), lambda bi, si: (bi, si, 0))
    return pl.pallas_call(
        _mixer_in_kernel,
        out_shape=(
            jax.ShapeDtypeStruct((b, s, FOX_W), bf16),
            jax.ShapeDtypeStruct((b, s, FOX_W), bf16),
            jax.ShapeDtypeStruct((b, FOX_W, s), bf16),
            jax.ShapeDtypeStruct((b, s, LANES), bf16),
            jax.ShapeDtypeStruct((b, s, LANES), bf16),
            jax.ShapeDtypeStruct((b, s, w), bf16),
        ),
        grid=(b, s // tm),
        in_specs=[tok(d), const((d, n)), const((1, LANES)), const((tm, tm)),
                  const((LANES, 2 * LANES)), const((SUBLANES, LANES)),
                  const((CONV_K, w)), const((1, w)), const((w, w)), const((1, w)),
                  const((w, w)), const((1, w)), const((1, w))],
        out_specs=(tok(FOX_W), tok(FOX_W), pl.BlockSpec((1, FOX_W, tm), lambda bi, si: (bi, 0, si)),
                   tok(LANES), tok(LANES), tok(w)),
        scratch_shapes=[pltpu.VMEM((SUBLANES, LANES), f32),
                        pltpu.VMEM((tm + 2 * SUBLANES, w), f32), pltpu.VMEM((SUBLANES, w), f32),
                        pltpu.VMEM((tm, w), f32), pltpu.VMEM((tm, w), f32), pltpu.VMEM((tm, w), f32)],
        compiler_params=pltpu.CompilerParams(
            dimension_semantics=("arbitrary", "arbitrary"),
            vmem_limit_bytes=48 * MIB),
        name="mixer_in",
    )(x1, w_in_p, bf_p, tri, place, ones, cw, cb, wa_bd, ba, wx_bd, bx, lam)


EXPM1_SERIES_BOUND = 0.0625


def _neg_expm1(y, exp_half_y):
    c = [-1.0 / math.factorial(n) for n in range(1, 6)]
    poly = c[4]
    for n in range(3, -1, -1):
        poly = poly * y + c[n]
    return jnp.where(y > -EXPM1_SERIES_BOUND, y * poly, 1.0 - exp_half_y * exp_half_y)


def _gelu_tanh(x):
    return 0.5 * x * (1.0 + jnp.tanh(math.sqrt(2.0 / math.pi) * (x + 0.044715 * (x * x * x))))


def _fox_attn_kernel(q_ref, aq_ref, k_ref, ak_ref, vt_ref, o_ref, m_ref, acc_ref):
    pair = pl.program_id(1)
    qi = pl.program_id(2)
    tq = q_ref.shape[1]
    tk = tq

    q2 = q_ref[0]
    aq = aq_ref[0]
    lane = lax.broadcasted_iota(jnp.int32, (tq, LANES), 1)
    zero = jnp.zeros((), bf16)
    one = jnp.ones((), bf16)

    qa = []
    for j in range(2):
        head = 2 * pair + j
        in_half = (lane >= j * FOX_HD) & (lane < (j + 1) * FOX_HD)
        in_head = (lane >= head * AUG_STRIDE) & (lane < (head + 1) * AUG_STRIDE)
        qa.append(jnp.concatenate([jnp.where(in_half, q2, zero), jnp.where(in_head, aq, zero)], axis=1).T)
        m_ref[j] = jnp.full((1, tq), NEG_BIG, f32)
        acc_ref[j] = jnp.zeros((LANES, tq), f32)

    chains = [(j, c) for j in range(2) for c in range(tq // QUERY_BLOCK)]

    def steps(tiles):
        todo = [(ti, j, c) for ti in range(len(tiles)) for j, c in chains]
        loaded = {}

        def scores(item):
            ti, j, c = item
            t, masked = tiles[ti]
            nk = min(tk, (c + 1) * QUERY_BLOCK) if masked else tk
            if (ti, nk) not in loaded:
                k0 = pl.multiple_of(t * tk, tk)
                ka = jnp.concatenate([k_ref[0, pl.ds(k0, nk), :], ak_ref[0, pl.ds(k0, nk), :]], axis=1)
                loaded[(ti, nk)] = (ka, vt_ref[0, :, pl.ds(k0, nk)])
            ka, vt2 = loaded[(ti, nk)]
            st = jnp.dot(ka, qa[j][:, c * QUERY_BLOCK:(c + 1) * QUERY_BLOCK], preferred_element_type=f32)
            return j, c, masked, vt2, st

        for j, c, masked, vt2, st in [scores(item) for item in todo]:
            cols = slice(c * QUERY_BLOCK, (c + 1) * QUERY_BLOCK)
            if masked:
                key = lax.broadcasted_iota(jnp.int32, st.shape, 0)
                qry = lax.broadcasted_iota(jnp.int32, st.shape, 1) + c * QUERY_BLOCK
                st = jnp.where(qry >= key, st, NEG_BIG)
            m_old = m_ref[j, :, cols]
            m_new = jnp.maximum(m_old, jnp.max(st, axis=0, keepdims=True))
            pt = jnp.exp2(st - m_new).astype(bf16)
            vrow = lax.broadcasted_iota(jnp.int32, vt2.shape, 0)
            own = (vrow >= j * FOX_HD) & (vrow < (j + 1) * FOX_HD)
            vtj = jnp.where(own, vt2, one)
            acc_ref[j, :, cols] = (jnp.exp2(m_old - m_new) * acc_ref[j, :, cols]
                                   + jnp.dot(vtj, pt, preferred_element_type=f32))
            m_ref[j, :, cols] = m_new

    def two_full_steps(u, carry):
        steps([(2 * u, False), (2 * u + 1, False)])
        return carry

    lax.fori_loop(0, qi // 2, two_full_steps, 0)

    @pl.when(qi % 2 == 1)
    def _():
        steps([(qi - 1, False), (qi, True)])

    @pl.when(qi % 2 == 0)
    def _():
        steps([(qi, True)])

    orow = lax.broadcasted_iota(jnp.int32, (LANES, tq), 0)
    outs = [acc_ref[j] / pltpu.roll(acc_ref[j], FOX_HD, axis=0) for j in range(2)]
    o_ref[0] = jnp.where(orow < FOX_HD, outs[0], outs[1]).T.astype(bf16)


def _fox_attn(q, aq, k, ak, vt, *, tq=512):
    b, s, w = q.shape
    n_pairs = w // LANES
    qspec = pl.BlockSpec((1, tq, LANES), lambda bi, pi, qi: (bi, qi, pi))
    aqspec = pl.BlockSpec((1, tq, LANES), lambda bi, pi, qi: (bi, qi, 0))
    kspec = pl.BlockSpec((1, s, LANES), lambda bi, pi, qi: (bi, 0, pi))
    akspec = pl.BlockSpec((1, s, LANES), lambda bi, pi, qi: (bi, 0, 0))
    vtspec = pl.BlockSpec((1, LANES, s), lambda bi, pi, qi: (bi, pi, 0))
    return pl.pallas_call(
        _fox_attn_kernel,
        out_shape=jax.ShapeDtypeStruct((b, s, w), bf16),
        grid=(b, n_pairs, s // tq),
        in_specs=[qspec, aqspec, kspec, akspec, vtspec],
        out_specs=qspec,
        scratch_shapes=[pltpu.VMEM((2, 1, tq), f32), pltpu.VMEM((2, LANES, tq), f32)],
        compiler_params=pltpu.CompilerParams(
            dimension_semantics=("parallel", "parallel", "arbitrary"),
            vmem_limit_bytes=40 * MIB),
        name="fox_attn",
    )(q, aq, k, ak, vt)


OUT_LN_ROW_CHUNK = 512


def _out_ln_kernel(x_ref, fox_ref, lru_ref, wf_ref, wl_ref, g_ref, b_ref, o_ref):
    for r0 in range(0, x_ref.shape[0], OUT_LN_ROW_CHUNK):
        rows = slice(r0, r0 + OUT_LN_ROW_CHUNK)
        mix = jnp.dot(fox_ref[rows, :], wf_ref[...], preferred_element_type=f32)
        mix = mix + jnp.dot(lru_ref[rows, :], wl_ref[...], preferred_element_type=f32)
        o_ref[rows, :] = _layer_norm(DN_ALPHA * x_ref[rows, :] + mix, g_ref[...], b_ref[...])


def _out_ln(x2d, fox2d, lru2d, wf, wl, g, b, *, tm=2048):
    t, d = x2d.shape
    const = lambda shape: pl.BlockSpec(shape, lambda i: (0,) * len(shape))
    return pl.pallas_call(
        _out_ln_kernel,
        out_shape=jax.ShapeDtypeStruct((t, d), f32),
        grid=(t // tm,),
        in_specs=[pl.BlockSpec((tm, d), lambda i: (i, 0)),
                  pl.BlockSpec((tm, FOX_W), lambda i: (i, 0)),
                  pl.BlockSpec((tm, LRU_W), lambda i: (i, 0)),
                  const((FOX_W, d)), const((LRU_W, d)), const((1, d)), const((1, d))],
        out_specs=pl.BlockSpec((tm, d), lambda i: (i, 0)),
        compiler_params=pltpu.CompilerParams(
            dimension_semantics=("parallel",),
            vmem_limit_bytes=56 * MIB),
        name="out_ln",
    )(x2d, fox2d, lru2d, wf, wl, g, b)


def _block_diag(w):
    m = jnp.zeros((LRU_W, LRU_W), f32)
    for i in range(LRU_BLOCKS):
        m = lax.dynamic_update_slice(m, w[i], (i * LRU_BW, i * LRU_BW))
    return m.astype(bf16)


def _decay_constants(tm):
    tri = (jnp.arange(tm)[:, None] >= jnp.arange(tm)[None, :]).astype(bf16)
    src = jnp.arange(LANES)[:, None]
    col = jnp.arange(2 * LANES)[None, :]
    part, head = src // FOX_HEADS, src % FOX_HEADS
    q_hit = (src < 3 * FOX_HEADS) & (col == head * AUG_STRIDE + part)
    k_hit = (src < 3 * FOX_HEADS) & (col == LANES + head * AUG_STRIDE + 3 + part)
    place = (q_hit.astype(f32) - k_hit.astype(f32)).astype(bf16)
    slot = jnp.arange(LANES) % AUG_STRIDE
    ones = jnp.zeros((SUBLANES, LANES), f32)
    ones = ones.at[0].set(((slot >= 3) & (slot < 6)).astype(f32))
    ones = ones.at[1].set((slot < 3).astype(f32))
    return tri, place, ones


def kernel(x, ffn1_w_gate, ffn1_w_up, ffn1_w_down, ln1_g, ln1_b, w_in, b_forget, conv_w, conv_b,
           rg_wa, rg_ba, rg_wx, rg_bx, lru_lambda, w_out, ln2_g, ln2_b,
           ffn2_w_gate, ffn2_w_up, ffn2_w_down, ln3_g, ln3_b):
    bsz, seq, d = x.shape
    tm_proj = 512
    tri, place, ones = _decay_constants(tm_proj)
    h = x
    for l in range(DEPTH):
        row = lambda p: p[l].reshape(1, -1)
        x1 = _ffn_ln(h.reshape(bsz * seq, d), ffn1_w_gate[l].astype(bf16), ffn1_w_up[l].astype(bf16),
                     ffn1_w_down[l].astype(bf16), row(ln1_g), row(ln1_b))
        w_in_p = jnp.pad(w_in[l], ((0, 0), (0, IN_COLS_PAD - IN_COLS))).astype(bf16)
        bf_p = jnp.pad(b_forget[l], (0, LANES - FOX_HEADS)).reshape(1, LANES)
        q, k, vt, aq, ak, lru = _mixer_in(
            x1.reshape(bsz, seq, d), w_in_p, bf_p, tri, place, ones, conv_w[l], row(conv_b),
            _block_diag(0.5 * rg_wa[l]), 0.5 * row(rg_ba), _block_diag(0.5 * rg_wx[l]), 0.5 * row(rg_bx),
            row(lru_lambda), tm=tm_proj)
        fox = _fox_attn(q, aq, k, ak, vt)
        w_out_b = w_out[l].astype(bf16)
        x2 = _out_ln(x1, fox.reshape(bsz * seq, FOX_W), lru.reshape(bsz * seq, LRU_W),
                     w_out_b[:FOX_W], w_out_b[FOX_W:], row(ln2_g), row(ln2_b))
        h = _ffn_ln(x2, ffn2_w_gate[l].astype(bf16), ffn2_w_up[l].astype(bf16),
                    ffn2_w_down[l].astype(bf16), row(ln3_g), row(ln3_b)).reshape(bsz, seq, d)
    return h
```

```python
import functools
import math

import jax
import jax.numpy as jnp
from jax import lax
from jax.experimental import pallas as pl
from jax.experimental.pallas import tpu as pltpu

f32 = jnp.float32
bf16 = jnp.bfloat16

D_MODEL = 1024
DEPTH = 1
FOX_W = 512
LRU_W = 512
FOX_HEADS = 8
FOX_HD = 64
LRU_BLOCKS = 8
LRU_BW = 64
LRU_C = 8.0
CONV_K = 4
D_FF = 4 * D_MODEL
LN_EPS = 1e-5
DN_ALPHA = (2.0 * DEPTH) ** 0.25
IN_COLS = 3 * FOX_W + 2 * LRU_W + FOX_HEADS

LANES = 128
SUBLANES = 8
MXU_DIM = 256

LOG2E = 1.4426950408889634
NEG_BIG = -1e30

AUG_STRIDE = LANES // FOX_HEADS
QUERY_BLOCK = MXU_DIM
IN_COLS_PAD = 3 * FOX_W + 2 * LRU_W + LANES

MIB = 1024 * 1024


def _layer_norm(y, g, b):
    mu = jnp.mean(y, axis=-1, keepdims=True)
    d = y - mu
    var = jnp.mean(d * d, axis=-1, keepdims=True)
    return d * lax.rsqrt(var + LN_EPS) * g + b


def _sigmoid(x):
    return 1.0 / (1.0 + jnp.exp(-x))


def _ffn_ln_kernel(x_ref, wg_ref, wu_ref, wd_ref, g_ref, b_ref, o_ref, yprev_ref, *, tc):
    i = pl.program_id(0)
    n = pl.num_programs(0)
    f = wg_ref.shape[1]

    @pl.when(i == 0)
    def _():
        yprev_ref[...] = jnp.zeros_like(yprev_ref)

    def finish_previous():
        o_ref[...] = _layer_norm(yprev_ref[...], g_ref[...], b_ref[...])

    @pl.when(i < n - 1)
    def _():
        finish_previous()
        x = x_ref[...]
        xb = x.astype(bf16)
        acc = None
        for c in range(f // tc):
            gate = jnp.dot(xb, wg_ref[:, c * tc:(c + 1) * tc], preferred_element_type=f32)
            up = jnp.dot(xb, wu_ref[:, c * tc:(c + 1) * tc], preferred_element_type=f32)
            h = (gate * _sigmoid(gate) * up).astype(bf16)
            part = jnp.dot(h, wd_ref[c * tc:(c + 1) * tc, :], preferred_element_type=f32)
            acc = part if acc is None else acc + part
        yprev_ref[...] = DN_ALPHA * x + 0.5 * acc

    @pl.when(i == n - 1)
    def _():
        finish_previous()


def _ffn_ln(x2d, wg, wu, wd, g, b, *, tm=1024, tc=512):
    t, d = x2d.shape
    f = wg.shape[1]
    n = t // tm
    resident = lambda shape: pl.BlockSpec(shape, lambda i: (0,) * len(shape), pipeline_mode=pl.Buffered(1))
    return pl.pallas_call(
        functools.partial(_ffn_ln_kernel, tc=tc),
        out_shape=jax.ShapeDtypeStruct((t, d), f32),
        grid=(n + 1,),
        in_specs=[
            pl.BlockSpec((tm, d), lambda i: (jnp.minimum(i, n - 1), 0)),
            resident((d, f)), resident((d, f)), resident((f, d)),
            resident((1, d)), resident((1, d)),
        ],
        out_specs=pl.BlockSpec((tm, d), lambda i: (jnp.maximum(i - 1, 0), 0)),
        scratch_shapes=[pltpu.VMEM((tm, d), f32)],
        compiler_params=pltpu.CompilerParams(
            dimension_semantics=("arbitrary",),
            vmem_limit_bytes=58 * MIB),
        name="ffn_ln",
    )(x2d, wg, wu, wd, g, b)


def _split3(x):
    hi = x.astype(bf16).astype(f32)
    r = x - hi
    mid = r.astype(bf16).astype(f32)
    lo = (r - mid).astype(bf16).astype(f32)
    return hi, mid, lo


def _pack3(x):
    hi, mid, lo = _split3(x)
    return (hi + pltpu.roll(mid, FOX_HEADS, axis=1) + pltpu.roll(lo, 2 * FOX_HEADS, axis=1)).astype(bf16)


Q_OFF, K_OFF, V_OFF = 0, FOX_W, 2 * FOX_W
LX_OFF, LG_OFF, FG_OFF = 3 * FOX_W, 3 * FOX_W + LRU_W, 3 * FOX_W + 2 * LRU_W


def _mixer_in_kernel(x_ref, w_ref, bf_ref, tri_ref, place_ref, ones_ref,
                     cw_ref, cb_ref, wa_ref, ba_ref, wx_ref, bx_ref, lam_ref,
                     q_ref, k_ref, vt_ref, aq_ref, ak_ref, lru_ref,
                     carry_ref, xpad_ref, h_ref, a_ref, b_ref, lg_ref):
    s_idx = pl.program_id(1)
    tm = x_ref.shape[1]
    pad = SUBLANES
    half = LRU_W // 2
    halves = [slice(hh * half, (hh + 1) * half) for hh in range(2)]

    @pl.when(s_idx == 0)
    def _():
        carry_ref[...] = jnp.zeros_like(carry_ref)
        xpad_ref[0:pad, :] = jnp.zeros((pad, LRU_W), f32)
        h_ref[...] = jnp.zeros_like(h_ref)

    xb = x_ref[0].astype(bf16)

    def project(col0, width):
        return jnp.dot(xb, w_ref[:, col0:col0 + width], preferred_element_type=f32)

    for hh in range(2):
        xpad_ref[pad:pad + tm, halves[hh]] = project(LX_OFF + hh * half, half)

    def conv(cols):
        u = cb_ref[:, cols] + cw_ref[CONV_K - 1:CONV_K, cols] * xpad_ref[pad:pad + tm, cols]
        for j in range(CONV_K - 1):
            off = pad - (CONV_K - 1) + j
            u = u + cw_ref[j:j + 1, cols] * xpad_ref[off:off + tm, cols]
        return u

    us = [conv(cols) for cols in halves]
    xpad_ref[0:pad, :] = xpad_ref[tm:tm + pad, :]

    ubs = [u.astype(bf16) for u in us]

    def gate(w_ref, bias_ref, hh):
        cols = halves[hh]
        return jnp.tanh(jnp.dot(ubs[hh], w_ref[cols, cols], preferred_element_type=f32)
                        + bias_ref[:, cols])

    def scan_groups(hh, t_r, t_gi):
        cols = halves[hh]
        lam = lam_ref[:, cols]
        softplus_neg_lam = jnp.maximum(-lam, 0.0) + jnp.log1p(jnp.exp(-jnp.abs(lam)))
        half_c = (-0.5 * LRU_C) * softplus_neg_lam
        log_a = half_c * t_r + half_c
        gi = 0.5 * t_gi + 0.5
        a = jnp.exp(log_a)
        one_minus_a2 = _neg_expm1(2.0 * log_a, a)
        root = jnp.where(one_minus_a2 > 0.0, one_minus_a2 * lax.rsqrt(one_minus_a2), 0.0)
        bterm = root * (gi * us[hh])
        a3 = a.reshape(tm // SUBLANES, SUBLANES, half)
        b3 = bterm.reshape(tm // SUBLANES, SUBLANES, half)
        row = lax.broadcasted_iota(jnp.int32, a3.shape, 1)
        for dist in (1, 2, 4):
            a_sh = pltpu.roll(a3, dist, axis=1)
            b_sh = pltpu.roll(b3, dist, axis=1)
            valid = row >= dist
            b3 = jnp.where(valid, a3 * b_sh + b3, b3)
            a3 = jnp.where(valid, a3 * a_sh, a3)
        a_ref[:, cols] = a3.reshape(tm, half)
        b_ref[:, cols] = b3.reshape(tm, half)

    r0 = gate(wa_ref, ba_ref, 0)
    q_ref[0] = (project(Q_OFF, FOX_W) * (LOG2E / math.sqrt(FOX_HD))).astype(bf16)
    gi0 = gate(wx_ref, bx_ref, 0)
    lg_ref[...] = project(LG_OFF, LRU_W)
    r1 = gate(wa_ref, ba_ref, 1)

    fg = project(FG_OFF, LANES) + bf_ref[...]
    lane = lax.broadcasted_iota(jnp.int32, fg.shape, 1)
    ls = (jnp.minimum(fg, 0.0) - jnp.log1p(jnp.exp(-jnp.abs(fg)))) * LOG2E
    ls = jnp.where(lane < FOX_HEADS, ls, 0.0)

    k_ref[0] = project(K_OFF, FOX_W).astype(bf16)
    gi1 = gate(wx_ref, bx_ref, 1)

    cum3 = jnp.dot(tri_ref[...], _pack3(ls), preferred_element_type=f32)
    cum = cum3 + pltpu.roll(cum3, LANES - FOX_HEADS, axis=1) + pltpu.roll(cum3, LANES - 2 * FOX_HEADS, axis=1)
    cum = jnp.where(lane < FOX_HEADS, cum, 0.0) + carry_ref[0:1, :]
    carry_ref[...] = jnp.broadcast_to(cum[tm - 1:tm, :], carry_ref.shape)

    vt_ref[0] = project(V_OFF, FOX_W).T.astype(bf16)

    aug = jnp.dot(_pack3(cum), place_ref[...], preferred_element_type=f32)
    aq_ref[0] = (aug[:, :LANES] + ones_ref[0:1, :]).astype(bf16)
    ak_ref[0] = (aug[:, LANES:] + ones_ref[1:2, :]).astype(bf16)

    scan_groups(0, r0, gi0)
    scan_groups(1, r1, gi1)

    def group(gi_, h):
        r0_ = pl.multiple_of(gi_ * SUBLANES, SUBLANES)
        out = a_ref[pl.ds(r0_, SUBLANES), :] * h + b_ref[pl.ds(r0_, SUBLANES), :]
        b_ref[pl.ds(r0_, SUBLANES), :] = out
        return jnp.broadcast_to(out[SUBLANES - 1:SUBLANES, :], (SUBLANES, LRU_W))

    h_ref[...] = lax.fori_loop(0, tm // SUBLANES, group, h_ref[...], unroll=8)
    lru_ref[0] = (_gelu_tanh(lg_ref[...]) * b_ref[...]).astype(bf16)


def _mixer_in(x1, w_in_p, bf_p, tri, place, ones, cw, cb, wa_bd, ba, wx_bd, bx, lam, *, tm=512):
    b, s, d = x1.shape
    n = w_in_p.shape[1]
    w = LRU_W
    const = lambda shape: pl.BlockSpec(shape, lambda bi, si: (0,) * len(shape))
    tok = lambda width: pl.BlockSpec((1, tm, width), lambda bi, si: (bi, si, 0))
    return pl.pallas_call(
        _mixer_in_kernel,
        out_shape=(
            jax.ShapeDtypeStruct((b, s, FOX_W), bf16),
            jax.ShapeDtypeStruct((b, s, FOX_W), bf16),
            jax.ShapeDtypeStruct((b, FOX_W, s), bf16),
            jax.ShapeDtypeStruct((b, s, LANES), bf16),
            jax.ShapeDtypeStruct((b, s, LANES), bf16),
            jax.ShapeDtypeStruct((b, s, w), bf16),
        ),
        grid=(b, s // tm),
        in_specs=[tok(d), const((d, n)), const((1, LANES)), const((tm, tm)),
                  const((LANES, 2 * LANES)), const((SUBLANES, LANES)),
                  const((CONV_K, w)), const((1, w)), const((w, w)), const((1, w)),
                  const((w, w)), const((1, w)), const((1, w))],
        out_specs=(tok(FOX_W), tok(FOX_W), pl.BlockSpec((1, FOX_W, tm), lambda bi, si: (bi, 0, si)),
                   tok(LANES), tok(LANES), tok(w)),
        scratch_shapes=[pltpu.VMEM((SUBLANES, LANES), f32),
                        pltpu.VMEM((tm + 2 * SUBLANES, w), f32), pltpu.VMEM((SUBLANES, w), f32),
                        pltpu.VMEM((tm, w), f32), pltpu.VMEM((tm, w), f32), pltpu.VMEM((tm, w), f32)],
        compiler_params=pltpu.CompilerParams(
            dimension_semantics=("arbitrary", "arbitrary"),
            vmem_limit_bytes=48 * MIB),
        name="mixer_in",
    )(x1, w_in_p, bf_p, tri, place, ones, cw, cb, wa_bd, ba, wx_bd, bx, lam)


EXPM1_SERIES_BOUND = 0.0625


def _neg_expm1(y, exp_half_y):
    c = [-1.0 / math.factorial(n) for n in range(1, 6)]
    poly = c[4]
    for n in range(3, -1, -1):
        poly = poly * y + c[n]
    return jnp.where(y > -EXPM1_SERIES_BOUND, y * poly, 1.0 - exp_half_y * exp_half_y)


def _gelu_tanh(x):
    return 0.5 * x * (1.0 + jnp.tanh(math.sqrt(2.0 / math.pi) * (x + 0.044715 * (x * x * x))))


def _fox_attn_kernel(q_ref, aq_ref, k_ref, ak_ref, vt_ref, o_ref, m_ref, acc_ref):
    pair = pl.program_id(1)
    qi = pl.program_id(2)
    tq = q_ref.shape[1]
    tk = tq

    q2 = q_ref[0]
    aq = aq_ref[0]
    lane = lax.broadcasted_iota(jnp.int32, (tq, LANES), 1)
    zero = jnp.zeros((), bf16)
    one = jnp.ones((), bf16)

    qa = []
    for j in range(2):
        head = 2 * pair + j
        in_half = (lane >= j * FOX_HD) & (lane < (j + 1) * FOX_HD)
        in_head = (lane >= head * AUG_STRIDE) & (lane < (head + 1) * AUG_STRIDE)
        qa.append(jnp.concatenate([jnp.where(in_half, q2, zero), jnp.where(in_head, aq, zero)], axis=1).T)
        m_ref[j] = jnp.full((1, tq), NEG_BIG, f32)
        acc_ref[j] = jnp.zeros((LANES, tq), f32)

    chains = [(j, c) for j in range(2) for c in range(tq // QUERY_BLOCK)]

    def steps(tiles):
        todo = [(ti, j, c) for ti in range(len(tiles)) for j, c in chains]
        loaded = {}

        def scores(item):
            ti, j, c = item
            t, masked = tiles[ti]
            nk = min(tk, (c + 1) * QUERY_BLOCK) if masked else tk
            if (ti, nk) not in loaded:
                k0 = pl.multiple_of(t * tk, tk)
                ka = jnp.concatenate([k_ref[0, pl.ds(k0, nk), :], ak_ref[0, pl.ds(k0, nk), :]], axis=1)
                loaded[(ti, nk)] = (ka, vt_ref[0, :, pl.ds(k0, nk)])
            ka, vt2 = loaded[(ti, nk)]
            st = jnp.dot(ka, qa[j][:, c * QUERY_BLOCK:(c + 1) * QUERY_BLOCK], preferred_element_type=f32)
            return j, c, masked, vt2, st

        for j, c, masked, vt2, st in [scores(item) for item in todo]:
            cols = slice(c * QUERY_BLOCK, (c + 1) * QUERY_BLOCK)
            if masked:
                key = lax.broadcasted_iota(jnp.int32, st.shape, 0)
                qry = lax.broadcasted_iota(jnp.int32, st.shape, 1) + c * QUERY_BLOCK
                st = jnp.where(qry >= key, st, NEG_BIG)
            m_old = m_ref[j, :, cols]
            m_new = jnp.maximum(m_old, jnp.max(st, axis=0, keepdims=True))
            pt = jnp.exp2(st - m_new).astype(bf16)
            vrow = lax.broadcasted_iota(jnp.int32, vt2.shape, 0)
            own = (vrow >= j * FOX_HD) & (vrow < (j + 1) * FOX_HD)
            vtj = jnp.where(own, vt2, one)
            acc_ref[j, :, cols] = (jnp.exp2(m_old - m_new) * acc_ref[j, :, cols]
                                   + jnp.dot(vtj, pt, preferred_element_type=f32))
            m_ref[j, :, cols] = m_new

    def two_full_steps(u, carry):
        steps([(2 * u, False), (2 * u + 1, False)])
        return carry

    lax.fori_loop(0, qi // 2, two_full_steps, 0)

    @pl.when(qi % 2 == 1)
    def _():
        steps([(qi - 1, False), (qi, True)])

    @pl.when(qi % 2 == 0)
    def _():
        steps([(qi, True)])

    orow = lax.broadcasted_iota(jnp.int32, (LANES, tq), 0)
    outs = [acc_ref[j] / pltpu.roll(acc_ref[j], FOX_HD, axis=0) for j in range(2)]
    o_ref[0] = jnp.where(orow < FOX_HD, outs[0], outs[1]).T.astype(bf16)


def _fox_attn(q, aq, k, ak, vt, *, tq=512):
    b, s, w = q.shape
    n_pairs = w // LANES
    qspec = pl.BlockSpec((1, tq, LANES), lambda bi, pi, qi: (bi, qi, pi))
    aqspec = pl.BlockSpec((1, tq, LANES), lambda bi, pi, qi: (bi, qi, 0))
    kspec = pl.BlockSpec((1, s, LANES), lambda bi, pi, qi: (bi, 0, pi))
    akspec = pl.BlockSpec((1, s, LANES), lambda bi, pi, qi: (bi, 0, 0))
    vtspec = pl.BlockSpec((1, LANES, s), lambda bi, pi, qi: (bi, pi, 0))
    return pl.pallas_call(
        _fox_attn_kernel,
        out_shape=jax.ShapeDtypeStruct((b, s, w), bf16),
        grid=(b, n_pairs, s // tq),
        in_specs=[qspec, aqspec, kspec, akspec, vtspec],
        out_specs=qspec,
        scratch_shapes=[pltpu.VMEM((2, 1, tq), f32), pltpu.VMEM((2, LANES, tq), f32)],
        compiler_params=pltpu.CompilerParams(
            dimension_semantics=("parallel", "parallel", "arbitrary"),
            vmem_limit_bytes=40 * MIB),
        name="fox_attn",
    )(q, aq, k, ak, vt)


OUT_LN_ROW_CHUNK = 256


def _out_ln_kernel(x_ref, fox_ref, lru_ref, wf_ref, wl_ref, g_ref, b_ref, o_ref):
    for r0 in range(0, x_ref.shape[0], OUT_LN_ROW_CHUNK):
        rows = slice(r0, r0 + OUT_LN_ROW_CHUNK)
        mix = jnp.dot(fox_ref[rows, :], wf_ref[...], preferred_element_type=f32)
        mix = mix + jnp.dot(lru_ref[rows, :], wl_ref[...], preferred_element_type=f32)
        o_ref[rows, :] = _layer_norm(DN_ALPHA * x_ref[rows, :] + mix, g_ref[...], b_ref[...])


def _out_ln(x2d, fox2d, lru2d, wf, wl, g, b, *, tm=2048):
    t, d = x2d.shape
    const = lambda shape: pl.BlockSpec(shape, lambda i: (0,) * len(shape))
    return pl.pallas_call(
        _out_ln_kernel,
        out_shape=jax.ShapeDtypeStruct((t, d), f32),
        grid=(t // tm,),
        in_specs=[pl.BlockSpec((tm, d), lambda i: (i, 0)),
                  pl.BlockSpec((tm, FOX_W), lambda i: (i, 0)),
                  pl.BlockSpec((tm, LRU_W), lambda i: (i, 0)),
                  const((FOX_W, d)), const((LRU_W, d)), const((1, d)), const((1, d))],
        out_specs=pl.BlockSpec((tm, d), lambda i: (i, 0)),
        compiler_params=pltpu.CompilerParams(
            dimension_semantics=("parallel",),
            vmem_limit_bytes=56 * MIB),
        name="out_ln",
    )(x2d, fox2d, lru2d, wf, wl, g, b)


def _block_diag(w):
    m = jnp.zeros((LRU_W, LRU_W), f32)
    for i in range(LRU_BLOCKS):
        m = lax.dynamic_update_slice(m, w[i], (i * LRU_BW, i * LRU_BW))
    return m.astype(bf16)


def _decay_constants(tm):
    tri = (jnp.arange(tm)[:, None] >= jnp.arange(tm)[None, :]).astype(bf16)
    src = jnp.arange(LANES)[:, None]
    col = jnp.arange(2 * LANES)[None, :]
    part, head = src // FOX_HEADS, src % FOX_HEADS
    q_hit = (src < 3 * FOX_HEADS) & (col == head * AUG_STRIDE + part)
    k_hit = (src < 3 * FOX_HEADS) & (col == LANES + head * AUG_STRIDE + 3 + part)
    place = (q_hit.astype(f32) - k_hit.astype(f32)).astype(bf16)
    slot = jnp.arange(LANES) % AUG_STRIDE
    ones = jnp.zeros((SUBLANES, LANES), f32)
    ones = ones.at[0].set(((slot >= 3) & (slot < 6)).astype(f32))
    ones = ones.at[1].set((slot < 3).astype(f32))
    return tri, place, ones


def kernel(x, ffn1_w_gate, ffn1_w_up, ffn1_w_down, ln1_g, ln1_b, w_in, b_forget, conv_w, conv_b,
           rg_wa, rg_ba, rg_wx, rg_bx, lru_lambda, w_out, ln2_g, ln2_b,
           ffn2_w_gate, ffn2_w_up, ffn2_w_down, ln3_g, ln3_b):
    bsz, seq, d = x.shape
    tm_proj = 512
    tri, place, ones = _decay_constants(tm_proj)
    h = x
    for l in range(DEPTH):
        row = lambda p: p[l].reshape(1, -1)
        x1 = _ffn_ln(h.reshape(bsz * seq, d), ffn1_w_gate[l].astype(bf16), ffn1_w_up[l].astype(bf16),
                     ffn1_w_down[l].astype(bf16), row(ln1_g), row(ln1_b))
        w_in_p = jnp.pad(w_in[l], ((0, 0), (0, IN_COLS_PAD - IN_COLS))).astype(bf16)
        bf_p = jnp.pad(b_forget[l], (0, LANES - FOX_HEADS)).reshape(1, LANES)
        q, k, vt, aq, ak, lru = _mixer_in(
            x1.reshape(bsz, seq, d), w_in_p, bf_p, tri, place, ones, conv_w[l], row(conv_b),
            _block_diag(0.5 * rg_wa[l]), 0.5 * row(rg_ba), _block_diag(0.5 * rg_wx[l]), 0.5 * row(rg_bx),
            row(lru_lambda), tm=tm_proj)
        fox = _fox_attn(q, aq, k, ak, vt)
        w_out_b = w_out[l].astype(bf16)
        x2 = _out_ln(x1, fox.reshape(bsz * seq, FOX_W), lru.reshape(bsz * seq, LRU_W),
                     w_out_b[:FOX_W], w_out_b[FOX_W:], row(ln2_g), row(ln2_b))
        h = _ffn_ln(x2, ffn2_w_gate[l].astype(bf16), ffn2_w_up[l].astype(bf16),
                    ffn2_w_down[l].astype(bf16), row(ln3_g), row(ln3_b)).reshape(bsz, seq, d)
    return h
```
